```python
import jax, jax.numpy as jnp
from jax import lax
import numpy as np

D_MODEL = 1024
BATCH = 16
SEQ = 2048
DEPTH = 1

D_CONV = 1024
CONV_WIDTH = 31
RWKV_HEAD = 64
RWKV_HEADS = D_MODEL // RWKV_HEAD
D_RWKV = RWKV_HEADS * RWKV_HEAD
DECAY_LORA = 64
AAA_LORA = 64
TS_WIDTH = 3 * D_RWKV + DECAY_LORA + AAA_LORA
SPLITS = (D_CONV, 2 * D_CONV, 3 * D_CONV, 3 * D_CONV + TS_WIDTH, 3 * D_CONV + TS_WIDTH + D_RWKV, 3 * D_CONV + TS_WIDTH + D_RWKV + D_MODEL)
D_IN = 3 * D_CONV + TS_WIDTH + D_RWKV + 2 * D_MODEL
RWKV_SPLITS = (D_RWKV, 2 * D_RWKV, 3 * D_RWKV, 3 * D_RWKV + DECAY_LORA)

RMS_EPS = 1e-6
LN_EPS = 1e-5
GN_EPS = 64e-5
L2_EPS = 1e-12

kernel_name = 'hybrid_conformer_conv_rwkv7_adaln'


def _rmsnorm(x, g):
    xf = x.astype(jnp.float32)
    y = xf * lax.rsqrt(jnp.mean(xf * xf, axis=-1, keepdims=True) + RMS_EPS)
    return (y * g.astype(jnp.float32)).astype(x.dtype)


def _layernorm(x, g, b, eps):
    xf = x.astype(jnp.float32)
    mu = jnp.mean(xf, axis=-1, keepdims=True)
    var = jnp.mean(jnp.square(xf - mu), axis=-1, keepdims=True)
    y = (xf - mu) * lax.rsqrt(var + eps) * g.astype(jnp.float32) + b.astype(jnp.float32)
    return y.astype(x.dtype)


def _causal_depthwise_conv(u, w, b):
    y = lax.conv_general_dilated(
        u, w[:, None, :].astype(u.dtype), window_strides=(1,),
        padding=((CONV_WIDTH - 1, 0),), dimension_numbers=('NWC', 'WIO', 'NWC'),
        feature_group_count=u.shape[-1])
    return y + b


def _token_shift(z, mu):
    prev = jnp.pad(z, ((0, 0), (1, 0), (0, 0)))[:, :-1]
    return z + mu * (prev - z)


def _rwkv7_scan(r, decay, k, v, kk, b):
    bsz, _, h, n = r.shape

    def step(state, inp):
        r_t, w_t, k_t, v_t, kk_t, b_t = inp
        sk = jnp.einsum('bhvk,bhk->bhv', state, kk_t)
        state = (state * w_t[:, :, None, :] - sk[..., None] * b_t[:, :, None, :]
                 + v_t[..., None] * k_t[:, :, None, :])
        return state, jnp.einsum('bhvk,bhk->bhv', state, r_t)

    xs = tuple(jnp.moveaxis(t, 1, 0) for t in (r, decay, k, v, kk, b))
    s0 = jnp.zeros((bsz, h, n, n), jnp.float32)
    _, out = lax.scan(step, s0, xs)
    return jnp.moveaxis(out, 0, 1)


def _rwkv7_branch(ts, og, mu, w0, w2, a0, a2, k_k, k_a, r_k, gn_g, gn_b, w_o):
    f32 = jnp.float32
    bsz, seq, _ = ts.shape
    hd = (bsz, seq, RWKV_HEADS, RWKV_HEAD)
    r, k, v, w_low, a_low = jnp.split(_token_shift(ts, mu), RWKV_SPLITS, axis=-1)
    w_pre = (w0 + jnp.tanh(w_low) @ w2).astype(f32)
    decay = jnp.exp(-jnp.exp(-jax.nn.softplus(-w_pre) - 0.5))
    a = jax.nn.sigmoid((a0 + a_low @ a2).astype(f32))
    kk = (k * k_k).astype(f32).reshape(hd)
    kk = kk / jnp.maximum(jnp.sqrt(jnp.sum(kk * kk, axis=-1, keepdims=True)), L2_EPS)
    k = (k.astype(f32) * (1.0 + (a - 1.0) * k_a.astype(f32))).reshape(hd)
    r = r.astype(f32).reshape(hd)
    v = v.astype(f32).reshape(hd)
    a = a.reshape(hd)
    o = _rwkv7_scan(r, decay.reshape(hd), k, v, kk, kk * a)
    o = _layernorm(o, gn_g.reshape(RWKV_HEADS, RWKV_HEAD), gn_b.reshape(RWKV_HEADS, RWKV_HEAD), GN_EPS)
    o = o + jnp.sum(r * k * r_k.astype(f32), axis=-1, keepdims=True) * v
    o = o.reshape(bsz, seq, D_RWKV).astype(og.dtype) * jax.nn.silu(og)
    return o @ w_o


def _conv_branch(val, gate, og, conv_k, conv_b, ln_g, ln_b, w_o):
    u = val * jax.nn.sigmoid(gate)
    u = _causal_depthwise_conv(u, conv_k, conv_b)
    u = _layernorm(u, ln_g, ln_b, LN_EPS)
    u = jax.nn.silu(u) * jax.nn.silu(og)
    return u @ w_o


def setup_inputs(seed: int = 0) -> dict:
    key = jax.random.key(seed)
    ks = jax.random.split(key, 24)
    f32 = jnp.float32
    nrm = lambda k, shape, s: jax.random.normal(k, shape, f32) * s
    L = DEPTH
    return {
        'x': nrm(ks[0], (BATCH, SEQ, D_MODEL), 1.0),
        'c': nrm(ks[1], (BATCH, D_MODEL), 1.0),
        'ada_w': nrm(ks[2], (L, D_MODEL, 3 * D_MODEL), 0.5 * D_MODEL ** -0.5),
        'ada_b': nrm(ks[3], (L, 3 * D_MODEL), 0.01),
        'norm_g': 1.0 + nrm(ks[4], (L, D_MODEL), 0.1),
        'w_in': nrm(ks[5], (L, D_MODEL, D_IN), D_MODEL ** -0.5),
        'conv_k': nrm(ks[6], (L, CONV_WIDTH, D_CONV), CONV_WIDTH ** -0.5),
        'conv_b': nrm(ks[7], (L, D_CONV), 0.01),
        'conv_ln_g': 1.0 + nrm(ks[8], (L, D_CONV), 0.1),
        'conv_ln_b': nrm(ks[9], (L, D_CONV), 0.01),
        'w_conv_out': nrm(ks[10], (L, D_CONV, D_MODEL), D_CONV ** -0.5),
        'rwkv_mu': jax.random.uniform(ks[11], (L, TS_WIDTH), f32, 0.0, 1.0),
        'rwkv_w0': jax.random.uniform(ks[12], (L, D_RWKV), f32, -6.0, -1.0),
        'rwkv_w2': nrm(ks[13], (L, DECAY_LORA, D_RWKV), 0.1 * DECAY_LORA ** -0.5),
        'rwkv_a0': nrm(ks[14], (L, D_RWKV), 0.1),
        'rwkv_a2': nrm(ks[15], (L, AAA_LORA, D_RWKV), 0.5 * AAA_LORA ** -0.5),
        'rwkv_k_k': 0.85 + nrm(ks[16], (L, D_RWKV), 0.05),
        'rwkv_k_a': 1.0 + nrm(ks[17], (L, D_RWKV), 0.05),
        'rwkv_r_k': nrm(ks[18], (L, RWKV_HEADS, RWKV_HEAD), 0.1),
        'rwkv_gn_g': 1.0 + nrm(ks[19], (L, D_RWKV), 0.1),
        'rwkv_gn_b': nrm(ks[20], (L, D_RWKV), 0.01),
        'w_rwkv_out': nrm(ks[21], (L, D_RWKV, D_MODEL), D_RWKV ** -0.5),
        'w_out': nrm(ks[22], (L, D_MODEL, D_MODEL), D_MODEL ** -0.5),
        'final_g': 1.0 + nrm(ks[23], (D_MODEL,), 0.1),
    }


def reference(x, c, ada_w, ada_b, norm_g, w_in, conv_k, conv_b, conv_ln_g, conv_ln_b, w_conv_out,
              rwkv_mu, rwkv_w0, rwkv_w2, rwkv_a0, rwkv_a2, rwkv_k_k, rwkv_k_a, rwkv_r_k,
              rwkv_gn_g, rwkv_gn_b, w_rwkv_out, w_out, final_g):
    for l in range(DEPTH):
        mod = jnp.einsum('bd,de->be', jax.nn.silu(c), ada_w[l]) + ada_b[l]
        shift, scale, gate = jnp.split(mod, 3, axis=-1)
        h = _rmsnorm(x, norm_g[l]) * (1.0 + scale[:, None, :]) + shift[:, None, :]
        p = jnp.einsum('bsd,de->bse', h, w_in[l])
        c_val, c_gate, c_og, ts, r_og, g_conv, g_rwkv = jnp.split(p, SPLITS, axis=-1)
        y_conv = _conv_branch(c_val, c_gate, c_og, conv_k[l], conv_b[l], conv_ln_g[l], conv_ln_b[l],
                              w_conv_out[l])
        y_rwkv = _rwkv7_branch(ts, r_og, rwkv_mu[l], rwkv_w0[l], rwkv_w2[l], rwkv_a0[l], rwkv_a2[l],
                               rwkv_k_k[l], rwkv_k_a[l], rwkv_r_k[l], rwkv_gn_g[l], rwkv_gn_b[l],
                               w_rwkv_out[l])
        m = jax.nn.sigmoid(g_conv) * y_conv + jax.nn.sigmoid(g_rwkv) * y_rwkv
        out = jnp.einsum('bsd,de->bse', m, w_out[l])
        x = x + gate[:, None, :] * out
    return _rmsnorm(x, final_g)
```

```python
import functools
import math

import jax
import jax.numpy as jnp
from jax import lax
from jax.experimental import pallas as pl
from jax.experimental.pallas import tpu as pltpu

F32 = jnp.float32
BF16 = jnp.bfloat16

RMS_EPS = 1e-6
LN_EPS = 1e-5
GN_EPS = 64e-5
L2_EPS = 1e-12

HEAD = 64
GROUP = 4
GW = HEAD * GROUP
CHUNK = 64
CONV_PAD = 32
SHIFT_PAD = 8

VMEM_LIMIT = 56 * 1024 * 1024


def _sigmoid(z):
    return 1.0 / (1.0 + jnp.exp(-z))


def _silu(z):
    return z * _sigmoid(z)


def _dot(a, b):
    return jnp.dot(a, b, preferred_element_type=F32)


def _modulated_rmsnorm(x, g, mod):
    ms = jnp.mean(x * x, axis=-1, keepdims=True)
    y = x * lax.rsqrt(ms + RMS_EPS) * g
    return y * (1.0 + mod[1:2, :]) + mod[0:1, :]


def _segsum(z, ones_bd):
    zb = z.astype(BF16)
    n = z.shape[-1] // GW
    return jnp.concatenate([_dot(zb[:, g * GW:(g + 1) * GW], ones_bd) for g in range(n)], axis=1)


def _mod_kernel(c_ref, w_ref, b_ref, o_ref):
    c = c_ref[...]
    s = _silu(c)
    o_ref[...] = jnp.dot(s, w_ref[...], preferred_element_type=F32,
                         precision=lax.Precision.HIGHEST) + b_ref[...]


def _mod_call(c, ada_w, ada_b):
    bsz, d = c.shape
    n = ada_w.shape[1]
    return pl.pallas_call(
        _mod_kernel,
        grid=(n // d,),
        in_specs=[pl.BlockSpec((bsz, d), lambda j: (0, 0)),
                  pl.BlockSpec((d, d), lambda j: (0, j)),
                  pl.BlockSpec((1, d), lambda j: (0, j))],
        out_specs=pl.BlockSpec((bsz, d), lambda j: (0, j)),
        out_shape=jax.ShapeDtypeStruct((bsz, n), F32),
        name="adaln_mod",
    )(c, ada_w, ada_b.reshape(1, n))


def _conv_kernel(x_ref, mod_ref, g_ref, wc_ref, ck_ref, cb_ref, lg_ref, lb_ref, wo_ref,
                 o_ref, ubuf, gbuf, ybuf, *, tm, rows, width):
    i = pl.program_id(1)
    d = x_ref.shape[-1]
    h = _modulated_rmsnorm(x_ref[...], g_ref[...], mod_ref[...]).astype(BF16)
    p = _dot(h, wc_ref[...])

    @pl.when(i == 0)
    def _():
        ubuf[0:CONV_PAD, :] = jnp.zeros((CONV_PAD, d), F32)

    ubuf[CONV_PAD:CONV_PAD + tm, :] = p[:, 0:d] * _sigmoid(p[:, d:2 * d])
    gbuf[...] = _silu(p[:, 2 * d:3 * d])
    base = CONV_PAD - (width - 1)
    for c in range(tm // rows):
        acc = jnp.broadcast_to(cb_ref[...], (rows, d))
        for j in range(width):
            acc = acc + ubuf[pl.ds(c * rows + base + j, rows), :] * ck_ref[pl.ds(j, 1), :]
        mu = jnp.mean(acc, axis=-1, keepdims=True)
        cen = acc - mu
        var = jnp.mean(cen * cen, axis=-1, keepdims=True)
        yn = cen * lax.rsqrt(var + LN_EPS) * lg_ref[...] + lb_ref[...]
        ybuf[pl.ds(c * rows, rows), :] = (_silu(yn) * gbuf[pl.ds(c * rows, rows), :]).astype(BF16)
    ubuf[0:CONV_PAD, :] = ubuf[tm:tm + CONV_PAD, :]
    o_ref[...] = _sigmoid(p[:, 3 * d:4 * d]) * _dot(ybuf[...], wo_ref[...])


def _conv_call(x, mod3, norm_g, wc, conv_k, conv_b, ln_g, ln_b, w_out, *, tm):
    bsz, seq, d = x.shape
    width = conv_k.shape[0]
    const = lambda b, i: (0, 0)
    kern = functools.partial(_conv_kernel, tm=tm, rows=32, width=width)
    return pl.pallas_call(
        kern,
        grid=(bsz, seq // tm),
        in_specs=[pl.BlockSpec((None, tm, d), lambda b, i: (b, i, 0)),
                  pl.BlockSpec((None, 3, d), lambda b, i: (b, 0, 0)),
                  pl.BlockSpec((1, d), const),
                  pl.BlockSpec(wc.shape, const),
                  pl.BlockSpec(conv_k.shape, const),
                  pl.BlockSpec((1, d), const),
                  pl.BlockSpec((1, d), const),
                  pl.BlockSpec((1, d), const),
                  pl.BlockSpec(w_out.shape, const)],
        out_specs=pl.BlockSpec((None, tm, d), lambda b, i: (b, i, 0)),
        out_shape=jax.ShapeDtypeStruct((bsz, seq, d), F32),
        scratch_shapes=[pltpu.VMEM((tm + CONV_PAD, d), F32),
                        pltpu.VMEM((tm, d), F32),
                        pltpu.VMEM((tm, d), BF16)],
        compiler_params=pltpu.CompilerParams(dimension_semantics=("arbitrary", "arbitrary"),
                                             vmem_limit_bytes=VMEM_LIMIT),
        name="conv_branch",
    )(x, mod3, norm_g, wc, conv_k, conv_b, ln_g, ln_b, w_out)


def _prep_kernel(x_ref, mod_ref, g_ref, wts_ref, wg_ref, mu_ref, w2_ref, w0_ref, a0_ref,
                 kk_ref, ka_ref, rk_ref, ones_ref,
                 r_out, k_out, v_out, kk_out, b_out, lw_out, bv_out, sr_out, sg_out,
                 tsbuf, *, tm):
    i = pl.program_id(1)
    d = x_ref.shape[-1]
    tsw = wts_ref.shape[-1]
    lora = tsw - 3 * d
    h = _modulated_rmsnorm(x_ref[...], g_ref[...], mod_ref[...]).astype(BF16)
    ts = _dot(h, wts_ref[...])
    gates = _dot(h, wg_ref[...])
    sr_out[...] = _silu(gates[:, 0:d]).astype(sr_out.dtype)
    sg_out[...] = _sigmoid(gates[:, d:2 * d]).astype(sg_out.dtype)

    @pl.when(i == 0)
    def _():
        tsbuf[0:SHIFT_PAD, :] = jnp.zeros((SHIFT_PAD, tsw), F32)

    tsbuf[SHIFT_PAD:SHIFT_PAD + tm, :] = ts
    prev = tsbuf[pl.ds(SHIFT_PAD - 1, tm), :]
    tsbuf[0:SHIFT_PAD, :] = tsbuf[tm:tm + SHIFT_PAD, :]
    z = ts + mu_ref[...] * (prev - ts)
    r = z[:, 0:d]
    k = z[:, d:2 * d]
    v = z[:, 2 * d:3 * d]
    low = z[:, 3 * d:tsw]
    lane = lax.broadcasted_iota(jnp.int32, low.shape, 1)
    low = jnp.where(lane < lora // 2, jnp.tanh(low), low)
    pre = _dot(low.astype(BF16), w2_ref[...])
    w_pre = w0_ref[...] + pre[:, 0:d]
    a = _sigmoid(a0_ref[...] + pre[:, d:2 * d])
    lw_out[...] = -_sigmoid(w_pre) * math.exp(-0.5)
    kkr = k * kk_ref[...]
    ss = _segsum(kkr * kkr, ones_ref[...])
    kkn = kkr / jnp.maximum(jnp.sqrt(ss), L2_EPS)
    k2 = k * (1.0 + (a - 1.0) * ka_ref[...])
    bonus = _segsum(r * k2 * rk_ref[...], ones_ref[...])
    r_out[...] = r.astype(r_out.dtype)
    k_out[...] = k2.astype(k_out.dtype)
    v_out[...] = v.astype(v_out.dtype)
    kk_out[...] = kkn.astype(kk_out.dtype)
    b_out[...] = (kkn * a).astype(b_out.dtype)
    bv_out[...] = (bonus * v).astype(bv_out.dtype)


def _prep_call(x, mod3, norm_g, wts, wg, mu, w2, w0, a0, k_k, k_a, r_k, ones_bd, *, tm):
    bsz, seq, d = x.shape
    const = lambda b, i: (0, 0)
    row = pl.BlockSpec((1, d), const)
    tile = pl.BlockSpec((None, tm, d), lambda b, i: (b, i, 0))
    act = jax.ShapeDtypeStruct((bsz, seq, d), F32)
    return pl.pallas_call(
        functools.partial(_prep_kernel, tm=tm),
        grid=(bsz, seq // tm),
        in_specs=[tile,
                  pl.BlockSpec((None, 3, d), lambda b, i: (b, 0, 0)),
                  row,
                  pl.BlockSpec(wts.shape, const),
                  pl.BlockSpec(wg.shape, const),
                  pl.BlockSpec(mu.shape, const),
                  pl.BlockSpec(w2.shape, const),
                  row, row, row, row, row,
                  pl.BlockSpec(ones_bd.shape, const)],
        out_specs=[tile] * 9,
        out_shape=[act] * 9,
        scratch_shapes=[pltpu.VMEM((tm + SHIFT_PAD, wts.shape[1]), F32)],
        compiler_params=pltpu.CompilerParams(dimension_semantics=("arbitrary", "arbitrary"),
                                             vmem_limit_bytes=VMEM_LIMIT),
        name="rwkv_prep",
    )(x, mod3, norm_g, wts, wg, mu, w2, w0, a0, k_k, k_a, r_k, ones_bd)


def _bdiag(y, bdm):
    return jnp.concatenate([y.astype(BF16)] * GROUP, axis=0) * bdm


def _scan_group(r, k, v, kk, b, e_l, e_lp, e_nl, g_c, h0, bdm, strict, incl, eye, bd_mask, diag_mask):
    c = CHUNK
    nt = (((1,), (1,)), ((), ()))
    tn = (((0,), (0,)), ((), ()))
    kd = k * e_nl
    bd = b * e_nl
    lhs = jnp.concatenate([kk * e_lp, r * e_l], axis=0).astype(BF16)
    a_b = lax.dot_general(lhs, _bdiag(bd, bdm), nt, preferred_element_type=F32)
    a_k = lax.dot_general(lhs, _bdiag(kd, bdm), nt, preferred_element_type=F32)
    ph = _dot(lhs, h0.astype(BF16))
    x = jnp.where(strict, -a_b[:c], 0.0)
    a_ak = jnp.where(strict, a_k[:c], 0.0)
    a_rb = jnp.where(incl, a_b[c:], 0.0)
    a_rk = jnp.where(incl, a_k[c:], 0.0)
    p = eye + x
    xp = _dot(x.astype(BF16), _bdiag(x, bdm))
    for _ in range(4):
        both = _dot(jnp.concatenate([p, xp], axis=0).astype(BF16), _bdiag(xp, bdm))
        p = p + both[:c]
        xp = both[c:]
    t = p + _dot(p.astype(BF16), _bdiag(xp, bdm))
    v_bd = _bdiag(v, bdm)
    ru = -(ph[:c] + _dot(a_ak.astype(BF16), v_bd))
    u = _dot(t.astype(BF16), _bdiag(ru, bdm))
    o = ph[c:] + _dot(a_rb.astype(BF16), _bdiag(u, bdm)) + _dot(a_rk.astype(BF16), v_bd)
    xs = jnp.concatenate([bd * g_c, kd * g_c], axis=0).astype(BF16)
    ys = jnp.concatenate([u, v], axis=0).astype(BF16)
    upd = lax.dot_general(xs, ys, tn, preferred_element_type=F32)
    g_col = jnp.sum(jnp.where(diag_mask, jnp.broadcast_to(g_c, (GW, GW)), 0.0), axis=-1, keepdims=True)
    h1 = g_col * h0 + jnp.where(bd_mask, upd, 0.0)
    return o, h1


def _scan_kernel(r_ref, k_ref, v_ref, kk_ref, b_ref, lw_ref, bdm_ref, tri_ref, o_ref, h_ref, *, tc):
    i = pl.program_id(1)
    d = r_ref.shape[-1]
    ngroups = d // GW

    @pl.when(i == 0)
    def _():
        h_ref[...] = jnp.zeros(h_ref.shape, F32)

    bdm = bdm_ref[...]
    tri = tri_ref[...]
    t_idx = lax.broadcasted_iota(jnp.int32, (CHUNK, GW), 0)
    s_idx = lax.broadcasted_iota(jnp.int32, (CHUNK, GW), 1) % HEAD
    strict = s_idx < t_idx
    incl = s_idx <= t_idx
    eye = (s_idx == t_idx).astype(F32)
    rr = lax.broadcasted_iota(jnp.int32, (GW, GW), 0)
    cc = lax.broadcasted_iota(jnp.int32, (GW, GW), 1)
    bd_mask = (rr // HEAD) == (cc // HEAD)
    diag_mask = rr == cc

    def chunk_body(ci, carry):
        rows = pl.ds(pl.multiple_of(ci * CHUNK, CHUNK), CHUNK)
        lw = lw_ref[rows, :]
        lw_hi = lw.astype(BF16)
        lw_lo = (lw - lw_hi.astype(F32)).astype(BF16)
        cum = _dot(tri, lw_hi) + _dot(tri, lw_lo)
        e_l = jnp.exp(cum)
        e_nl = jnp.exp(-cum)
        e_lp = jnp.exp(cum - lw)
        g_c = e_l[CHUNK - 1:CHUNK, :]
        for g in range(ngroups):
            ln = slice(g * GW, (g + 1) * GW)
            o, h1 = _scan_group(r_ref[rows, ln].astype(F32), k_ref[rows, ln].astype(F32),
                                v_ref[rows, ln].astype(F32), kk_ref[rows, ln].astype(F32),
                                b_ref[rows, ln].astype(F32),
                                e_l[:, ln], e_lp[:, ln], e_nl[:, ln], g_c[:, ln], h_ref[g],
                                bdm, strict, incl, eye, bd_mask, diag_mask)
            h_ref[g] = h1
            o_ref[rows, ln] = o.astype(o_ref.dtype)
        return carry

    lax.fori_loop(0, tc // CHUNK, chunk_body, 0)


def _scan_call(r, k, v, kk, b, lw, bdm, tri, *, tc):
    bsz, seq, d = r.shape
    tile = pl.BlockSpec((None, tc, d), lambda bi, i: (bi, i, 0))
    const = lambda bi, i: (0, 0)
    return pl.pallas_call(
        functools.partial(_scan_kernel, tc=tc),
        grid=(bsz, seq // tc),
        in_specs=[tile] * 6 + [pl.BlockSpec(bdm.shape, const), pl.BlockSpec(tri.shape, const)],
        out_specs=tile,
        out_shape=jax.ShapeDtypeStruct((bsz, seq, d), F32),
        scratch_shapes=[pltpu.VMEM((d // GW, GW, GW), F32)],
        compiler_params=pltpu.CompilerParams(dimension_semantics=("arbitrary", "arbitrary"),
                                             vmem_limit_bytes=VMEM_LIMIT),
        name="rwkv_scan",
    )(r, k, v, kk, b, lw, bdm, tri)


def _post_kernel(o_ref, bv_ref, sr_ref, sg_ref, mc_ref, x_ref, mod_ref, gg_ref, gb_ref, ones_ref,
                 wr_ref, wo_ref, fg_ref, out_ref, *, last):
    o = o_ref[...]
    inv = 1.0 / HEAD
    mu = _segsum(o, ones_ref[...]) * inv
    cen = o - mu
    var = _segsum(cen * cen, ones_ref[...]) * inv
    on = cen * lax.rsqrt(var + GN_EPS) * gg_ref[...] + gb_ref[...]
    y = ((on + bv_ref[...]) * sr_ref[...]).astype(BF16)
    m = sg_ref[...] * _dot(y, wr_ref[...]) + mc_ref[...]
    out = _dot(m.astype(BF16), wo_ref[...])
    xn = x_ref[...] + mod_ref[2:3, :] * out
    if last:
        ms = jnp.mean(xn * xn, axis=-1, keepdims=True)
        xn = xn * lax.rsqrt(ms + RMS_EPS) * fg_ref[...]
    out_ref[...] = xn


def _post_call(o, bv, sr, sg, mc, x, mod3, gn_g, gn_b, ones_bd, w_r, w_o, final_g, *, tm, last):
    bsz, seq, d = x.shape
    const = lambda b, i: (0, 0)
    row = pl.BlockSpec((1, d), const)
    tile = pl.BlockSpec((None, tm, d), lambda b, i: (b, i, 0))
    return pl.pallas_call(
        functools.partial(_post_kernel, last=last),
        grid=(bsz, seq // tm),
        in_specs=[tile] * 6 + [pl.BlockSpec((None, 3, d), lambda b, i: (b, 0, 0)), row, row,
                               pl.BlockSpec(ones_bd.shape, const),
                               pl.BlockSpec(w_r.shape, const), pl.BlockSpec(w_o.shape, const), row],
        out_specs=tile,
        out_shape=jax.ShapeDtypeStruct((bsz, seq, d), F32),
        compiler_params=pltpu.CompilerParams(dimension_semantics=("arbitrary", "arbitrary"),
                                             vmem_limit_bytes=VMEM_LIMIT),
        name="merge_out",
    )(o, bv, sr, sg, mc, x, mod3, gn_g, gn_b, ones_bd, w_r, w_o, final_g)


def kernel(x, c, ada_w, ada_b, norm_g, w_in, conv_k, conv_b, conv_ln_g, conv_ln_b, w_conv_out,
           rwkv_mu, rwkv_w0, rwkv_w2, rwkv_a0, rwkv_a2, rwkv_k_k, rwkv_k_a, rwkv_r_k,
           rwkv_gn_g, rwkv_gn_b, w_rwkv_out, w_out, final_g):
    depth = ada_w.shape[0]
    bsz, seq, d = x.shape
    lora_w = rwkv_w2.shape[1]
    lora_a = rwkv_a2.shape[1]
    tsw = 3 * d + lora_w + lora_a
    assert rwkv_w2.shape[2] == d and lora_w == lora_a and d % GW == 0 and seq % 256 == 0
    row = lambda t: t.reshape(1, -1).astype(F32)

    idx = jnp.arange(GW) // HEAD
    bdm = (idx[:, None] == idx[None, :]).astype(BF16)
    tri = (jnp.arange(CHUNK)[None, :] <= jnp.arange(CHUNK)[:, None]).astype(BF16)

    for l in range(depth):
        wl = w_in[l]
        wc = jnp.concatenate([wl[:, 0:3 * d], wl[:, 4 * d + tsw:5 * d + tsw]], axis=1).astype(BF16)
        wts = wl[:, 3 * d:3 * d + tsw].astype(BF16)
        wg = jnp.concatenate([wl[:, 3 * d + tsw:4 * d + tsw], wl[:, 5 * d + tsw:6 * d + tsw]], axis=1).astype(BF16)
        zeros = jnp.zeros((lora_w, d), F32)
        w2 = jnp.concatenate([jnp.concatenate([rwkv_w2[l], zeros], axis=1),
                              jnp.concatenate([zeros, rwkv_a2[l]], axis=1)], axis=0).astype(BF16)

        mod3 = _mod_call(c, ada_w[l], ada_b[l]).reshape(bsz, 3, d)
        g = row(norm_g[l])
        m_conv = _conv_call(x, mod3, g, wc, conv_k[l], row(conv_b[l]), row(conv_ln_g[l]), row(conv_ln_b[l]),
                            w_conv_out[l].astype(BF16), tm=256)
        r, k2, v, kk, b, lw, bv, sr, sg = _prep_call(
            x, mod3, g, wts, wg, row(rwkv_mu[l]), w2, row(rwkv_w0[l]), row(rwkv_a0[l]),
            row(rwkv_k_k[l]), row(rwkv_k_a[l]), row(rwkv_r_k[l]), bdm, tm=256)
        o = _scan_call(r, k2, v, kk, b, lw, bdm, tri, tc=256)
        x = _post_call(o, bv, sr, sg, m_conv, x, mod3, row(rwkv_gn_g[l]), row(rwkv_gn_b[l]), bdm,
                       w_rwkv_out[l].astype(BF16), w_out[l].astype(BF16), row(final_g),
                       tm=256, last=(l == depth - 1))
    return x
```

```python
import functools
import math

import jax
import jax.numpy as jnp
from jax import lax
from jax.experimental import pallas as pl
from jax.experimental.pallas import tpu as pltpu

F32 = jnp.float32
BF16 = jnp.bfloat16

RMS_EPS = 1e-6
LN_EPS = 1e-5
GN_EPS = 64e-5
L2_EPS = 1e-12

HEAD = 64
GROUP = 4
GW = HEAD * GROUP
CHUNK = 64
CONV_PAD = 32
SHIFT_PAD = 8
CONV_ROWS = 128
CONV_LANES = 128

VMEM_LIMIT = 56 * 1024 * 1024


def _sigmoid(z):
    return 1.0 / (1.0 + jnp.exp(-z))


def _silu(z):
    return z * _sigmoid(z)


def _dot(a, b):
    return jnp.dot(a, b, preferred_element_type=F32)


def _modulated_rmsnorm(x, g, mod):
    ms = jnp.mean(x * x, axis=-1, keepdims=True)
    y = x * lax.rsqrt(ms + RMS_EPS) * g
    return y * (1.0 + mod[1:2, :]) + mod[0:1, :]


def _segsum(z, ones_bd):
    zb = z.astype(BF16)
    n = z.shape[-1] // GW
    return jnp.concatenate([_dot(zb[:, g * GW:(g + 1) * GW], ones_bd) for g in range(n)], axis=1)


def _mod_kernel(c_ref, w_ref, b_ref, o_ref):
    c = c_ref[...]
    s = _silu(c)
    o_ref[...] = jnp.dot(s, w_ref[...], preferred_element_type=F32,
                         precision=lax.Precision.HIGHEST) + b_ref[...]


def _mod_call(c, ada_w, ada_b):
    bsz, d = c.shape
    n = ada_w.shape[1]
    return pl.pallas_call(
        _mod_kernel,
        grid=(n // d,),
        in_specs=[pl.BlockSpec((bsz, d), lambda j: (0, 0)),
                  pl.BlockSpec((d, d), lambda j: (0, j)),
                  pl.BlockSpec((1, d), lambda j: (0, j))],
        out_specs=pl.BlockSpec((bsz, d), lambda j: (0, j)),
        out_shape=jax.ShapeDtypeStruct((bsz, n), F32),
        name="adaln_mod",
    )(c, ada_w, ada_b.reshape(1, n))


def _conv_kernel(x_ref, mod_ref, g_ref, wc_ref, ck_ref, cb_ref, lg_ref, lb_ref, wo_ref,
                 o_ref, ubuf, gbuf, ybuf, cbuf, mbuf, hbuf, *, tm, sub, rows, width):
    i = pl.program_id(1)
    d = x_ref.shape[-1]

    @pl.when(i == 0)
    def _():
        ubuf[0:CONV_PAD, :] = jnp.zeros((CONV_PAD, d), F32)

    def project(t0):
        sl = pl.ds(t0, sub)
        hbuf[...] = _modulated_rmsnorm(x_ref[sl, :], g_ref[...], mod_ref[...]).astype(BF16)

        def piece(n):
            cols = [pl.ds(s * d + n * GW, GW) for s in range(4)]
            h = hbuf[...]
            val, gate, og, mg = [_dot(h, wc_ref[:, cs]) for cs in cols]
            ubuf[pl.ds(CONV_PAD + t0, sub), cols[0]] = val * _sigmoid(gate)
            gbuf[sl, cols[0]] = _silu(og)
            mbuf[sl, cols[0]] = _sigmoid(mg)

        return [functools.partial(piece, n) for n in range(d // GW)]

    def taps(t0):
        def piece(lb):
            ln = slice(lb * CONV_LANES, (lb + 1) * CONV_LANES)
            for c in range(sub // CONV_ROWS):
                first = CONV_PAD + t0 + c * CONV_ROWS
                acc = jnp.broadcast_to(cb_ref[:, ln], (CONV_ROWS, CONV_LANES))
                for r in range(8):
                    lead = 0 if r == 0 else 8
                    z = None
                    for q in range((width - 1 - r) // 8 + 1):
                        j = width - 1 - (8 * q + r)
                        term = ubuf[pl.ds(first - lead - 8 * q, CONV_ROWS + lead), ln] * ck_ref[pl.ds(j, 1), ln]
                        z = term if z is None else z + term
                    acc = acc + (z if r == 0 else pltpu.roll(z, r, axis=0)[lead:, :])
                cbuf[pl.ds(t0 + c * CONV_ROWS, CONV_ROWS), ln] = acc

        return [functools.partial(piece, lb) for lb in range(d // CONV_LANES)]

    def finish(t0):
        for c in range(sub // rows):
            sl = pl.ds(t0 + c * rows, rows)
            acc = cbuf[sl, :]
            mu = jnp.mean(acc, axis=-1, keepdims=True)
            cen = acc - mu
            var = jnp.mean(cen * cen, axis=-1, keepdims=True)
            yn = cen * lax.rsqrt(var + LN_EPS) * lg_ref[...] + lb_ref[...]
            ybuf[sl, :] = (_silu(yn) * gbuf[sl, :]).astype(BF16)
        sl = pl.ds(t0, sub)
        o_ref[sl, :] = (mbuf[sl, :] * _dot(ybuf[sl, :], wo_ref[...])).astype(o_ref.dtype)

    starts = list(range(0, tm, sub))
    for run in project(starts[0]):
        run()
    for n, t0 in enumerate(starts):
        vec = taps(t0)
        mat = project(starts[n + 1]) if n + 1 < len(starts) else []
        per = len(vec) // max(len(mat), 1)
        for m, run in enumerate(vec):
            if m % per == 0 and m // per < len(mat):
                mat[m // per]()
            run()
        finish(t0)
    ubuf[0:CONV_PAD, :] = ubuf[tm:tm + CONV_PAD, :]


def _conv_call(x, mod3, norm_g, wc, conv_k, conv_b, ln_g, ln_b, w_out, *, tm, sub):
    bsz, seq, d = x.shape
    width = conv_k.shape[0]
    const = lambda b, i: (0, 0)
    kern = functools.partial(_conv_kernel, tm=tm, sub=sub, rows=32, width=width)
    return pl.pallas_call(
        kern,
        grid=(bsz, seq // tm),
        in_specs=[pl.BlockSpec((None, tm, d), lambda b, i: (b, i, 0)),
                  pl.BlockSpec((None, 3, d), lambda b, i: (b, 0, 0)),
                  pl.BlockSpec((1, d), const),
                  pl.BlockSpec(wc.shape, const),
                  pl.BlockSpec(conv_k.shape, const),
                  pl.BlockSpec((1, d), const),
                  pl.BlockSpec((1, d), const),
                  pl.BlockSpec((1, d), const),
                  pl.BlockSpec(w_out.shape, const)],
        out_specs=pl.BlockSpec((None, tm, d), lambda b, i: (b, i, 0)),
        out_shape=jax.ShapeDtypeStruct((bsz, seq, d), BF16),
        scratch_shapes=[pltpu.VMEM((tm + CONV_PAD, d), F32),
                        pltpu.VMEM((tm, d), F32),
                        pltpu.VMEM((tm, d), BF16),
                        pltpu.VMEM((tm, d), F32),
                        pltpu.VMEM((tm, d), F32),
                        pltpu.VMEM((sub, d), BF16)],
        compiler_params=pltpu.CompilerParams(dimension_semantics=("arbitrary", "arbitrary"),
                                             vmem_limit_bytes=VMEM_LIMIT),
        name="conv_branch",
    )(x, mod3, norm_g, wc, conv_k, conv_b, ln_g, ln_b, w_out)


def _prep_kernel(x_ref, mod_ref, g_ref, wts_ref, wg_ref, mu_ref, w2_ref, w0_ref, a0_ref,
                 kk_ref, ka_ref, rk_ref, ones_ref,
                 r_out, k_out, v_out, kk_out, b_out, lw_out, bv_out, sr_out, sg_out,
                 tsbuf, *, tm):
    i = pl.program_id(1)
    d = x_ref.shape[-1]
    tsw = wts_ref.shape[-1]
    lora = tsw - 3 * d
    h = _modulated_rmsnorm(x_ref[...], g_ref[...], mod_ref[...]).astype(BF16)
    ts = _dot(h, wts_ref[...])
    gates = _dot(h, wg_ref[...])
    sr_out[...] = _silu(gates[:, 0:d]).astype(sr_out.dtype)
    sg_out[...] = _sigmoid(gates[:, d:2 * d]).astype(sg_out.dtype)

    @pl.when(i == 0)
    def _():
        tsbuf[0:SHIFT_PAD, :] = jnp.zeros((SHIFT_PAD, tsw), F32)

    tsbuf[SHIFT_PAD:SHIFT_PAD + tm, :] = ts
    prev = tsbuf[pl.ds(SHIFT_PAD - 1, tm), :]
    tsbuf[0:SHIFT_PAD, :] = tsbuf[tm:tm + SHIFT_PAD, :]
    z = ts + mu_ref[...] * (prev - ts)
    r = z[:, 0:d]
    k = z[:, d:2 * d]
    v = z[:, 2 * d:3 * d]
    low = z[:, 3 * d:tsw]
    lane = lax.broadcasted_iota(jnp.int32, low.shape, 1)
    low = jnp.where(lane < lora // 2, jnp.tanh(low), low)
    pre = _dot(low.astype(BF16), w2_ref[...])
    w_pre = w0_ref[...] + pre[:, 0:d]
    a = _sigmoid(a0_ref[...] + pre[:, d:2 * d])
    lw_out[...] = -_sigmoid(w_pre) * math.exp(-0.5)
    kkr = k * kk_ref[...]
    ss = _segsum(kkr * kkr, ones_ref[...])
    kkn = kkr / jnp.maximum(jnp.sqrt(ss), L2_EPS)
    k2 = k * (1.0 + (a - 1.0) * ka_ref[...])
    bonus = _segsum(r * k2 * rk_ref[...], ones_ref[...])
    r_out[...] = r.astype(r_out.dtype)
    k_out[...] = k2.astype(k_out.dtype)
    v_out[...] = v.astype(v_out.dtype)
    kk_out[...] = kkn.astype(kk_out.dtype)
    b_out[...] = (kkn * a).astype(b_out.dtype)
    bv_out[...] = (bonus * v).astype(bv_out.dtype)


def _prep_call(x, mod3, norm_g, wts, wg, mu, w2, w0, a0, k_k, k_a, r_k, ones_bd, *, tm):
    bsz, seq, d = x.shape
    const = lambda b, i: (0, 0)
    row = pl.BlockSpec((1, d), const)
    tile = pl.BlockSpec((None, tm, d), lambda b, i: (b, i, 0))
    act = jax.ShapeDtypeStruct((bsz, seq, d), BF16)
    logw = jax.ShapeDtypeStruct((bsz, seq, d), F32)
    return pl.pallas_call(
        functools.partial(_prep_kernel, tm=tm),
        grid=(bsz, seq // tm),
        in_specs=[tile,
                  pl.BlockSpec((None, 3, d), lambda b, i: (b, 0, 0)),
                  row,
                  pl.BlockSpec(wts.shape, const),
                  pl.BlockSpec(wg.shape, const),
                  pl.BlockSpec(mu.shape, const),
                  pl.BlockSpec(w2.shape, const),
                  row, row, row, row, row,
                  pl.BlockSpec(ones_bd.shape, const)],
        out_specs=[tile] * 9,
        out_shape=[act] * 5 + [logw] + [act] * 3,
        scratch_shapes=[pltpu.VMEM((tm + SHIFT_PAD, wts.shape[1]), F32)],
        compiler_params=pltpu.CompilerParams(dimension_semantics=("arbitrary", "arbitrary"),
                                             vmem_limit_bytes=VMEM_LIMIT),
        name="rwkv_prep",
    )(x, mod3, norm_g, wts, wg, mu, w2, w0, a0, k_k, k_a, r_k, ones_bd)


def _bdiag(y, bdm):
    return jnp.concatenate([y.astype(BF16)] * GROUP, axis=0) * bdm


def _scan_streams(streams, bdm, strict, incl, eye, bd_mask, diag_mask):
    c = CHUNK
    n = len(streams)
    nt = (((1,), (1,)), ((), ()))
    tn = (((0,), (0,)), ((), ()))
    each = lambda f, *cols: [f(*args) for args in zip(*cols)]
    r, k, v, kk, b, e_l, e_lp, e_nl, g_c, h0 = [list(col) for col in zip(*streams)]
    kd = each(lambda k_, e: k_ * e, k, e_nl)
    bd = each(lambda b_, e: b_ * e, b, e_nl)
    lhs = each(lambda kk_, ep, r_, e: jnp.concatenate([kk_ * ep, r_ * e], axis=0).astype(BF16), kk, e_lp, r, e_l)
    a_b = each(lambda l_, y: lax.dot_general(l_, _bdiag(y, bdm), nt, preferred_element_type=F32), lhs, bd)
    a_k = each(lambda l_, y: lax.dot_general(l_, _bdiag(y, bdm), nt, preferred_element_type=F32), lhs, kd)
    x = each(lambda a: jnp.where(strict, -a[:c], 0.0), a_b)
    xp = each(lambda x_: _dot(x_.astype(BF16), _bdiag(x_, bdm)), x)
    ph = each(lambda l_, h: _dot(l_, h.astype(BF16)), lhs, h0)
    p = each(lambda x_: eye + x_, x)
    v_bd = each(lambda v_: _bdiag(v_, bdm), v)
    a_ak = each(lambda a: jnp.where(strict, a[:c], 0.0).astype(BF16), a_k)
    a_rk = each(lambda a: jnp.where(incl, a[c:], 0.0).astype(BF16), a_k)
    a_rb = each(lambda a: jnp.where(incl, a[c:], 0.0).astype(BF16), a_b)
    for _ in range(4):
        both = each(lambda p_, xp_: _dot(jnp.concatenate([p_, xp_], axis=0).astype(BF16), _bdiag(xp_, bdm)), p, xp)
        p = each(lambda p_, bo: p_ + bo[:c], p, both)
        xp = each(lambda bo: bo[c:], both)
    av = each(lambda a, vb: _dot(a, vb), a_ak, v_bd)
    t = each(lambda p_, xp_: p_ + _dot(p_.astype(BF16), _bdiag(xp_, bdm)), p, xp)
    ov = each(lambda a, vb: _dot(a, vb), a_rk, v_bd)
    ru = each(lambda ph_, av_: -(ph_[:c] + av_), ph, av)
    u = each(lambda t_, ru_: _dot(t_.astype(BF16), _bdiag(ru_, bdm)), t, ru)
    o = each(lambda ph_, a, u_, ov_: ph_[c:] + _dot(a, _bdiag(u_, bdm)) + ov_, ph, a_rb, u, ov)
    xs = each(lambda bd_, kd_, g: jnp.concatenate([bd_ * g, kd_ * g], axis=0).astype(BF16), bd, kd, g_c)
    ys = each(lambda u_, v_: jnp.concatenate([u_, v_], axis=0).astype(BF16), u, v)
    upd = each(lambda xs_, ys_: lax.dot_general(xs_, ys_, tn, preferred_element_type=F32), xs, ys)
    g_col = each(lambda g: jnp.sum(jnp.where(diag_mask, jnp.broadcast_to(g, (GW, GW)), 0.0), axis=-1, keepdims=True),
                 g_c)
    h1 = each(lambda g, h, up: g * h + jnp.where(bd_mask, up, 0.0), g_col, h0, upd)
    return [(o[i], h1[i]) for i in range(n)]


def _scan_kernel(r_ref, k_ref, v_ref, kk_ref, b_ref, lw_ref, bdm_ref, tri_ref, o_ref, h_ref, *, tc, nb):
    i = pl.program_id(1)
    d = r_ref.shape[-1]
    ngroups = d // GW

    @pl.when(i == 0)
    def _():
        h_ref[...] = jnp.zeros(h_ref.shape, F32)

    bdm = bdm_ref[...]
    tri = tri_ref[...]
    t_idx = lax.broadcasted_iota(jnp.int32, (CHUNK, GW), 0)
    s_idx = lax.broadcasted_iota(jnp.int32, (CHUNK, GW), 1) % HEAD
    strict = s_idx < t_idx
    incl = s_idx <= t_idx
    eye = (s_idx == t_idx).astype(F32)
    rr = lax.broadcasted_iota(jnp.int32, (GW, GW), 0)
    cc = lax.broadcasted_iota(jnp.int32, (GW, GW), 1)
    bd_mask = (rr // HEAD) == (cc // HEAD)
    diag_mask = rr == cc

    def chunk_body(ci, carry):
        rows = pl.ds(pl.multiple_of(ci * CHUNK, CHUNK), CHUNK)
        lanes = [slice(g * GW, (g + 1) * GW) for g in range(ngroups)]
        streams = []
        for bi in range(nb):
            lw = lw_ref[bi, rows, :]
            lw_hi = lw.astype(BF16)
            lw_lo = (lw - lw_hi.astype(F32)).astype(BF16)
            cum = _dot(tri, lw_hi) + _dot(tri, lw_lo)
            e_l = jnp.exp(cum)
            e_nl = jnp.exp(-cum)
            e_lp = jnp.exp(cum - lw)
            g_c = e_l[CHUNK - 1:CHUNK, :]
            streams += [(r_ref[bi, rows, ln].astype(F32), k_ref[bi, rows, ln].astype(F32),
                         v_ref[bi, rows, ln].astype(F32), kk_ref[bi, rows, ln].astype(F32),
                         b_ref[bi, rows, ln].astype(F32),
                         e_l[:, ln], e_lp[:, ln], e_nl[:, ln], g_c[:, ln], h_ref[bi * ngroups + g])
                        for g, ln in enumerate(lanes)]
        res = _scan_streams(streams, bdm, strict, incl, eye, bd_mask, diag_mask)
        for bi in range(nb):
            for g, ln in enumerate(lanes):
                o, h1 = res[bi * ngroups + g]
                h_ref[bi * ngroups + g] = h1
                o_ref[bi, rows, ln] = o.astype(o_ref.dtype)
        return carry

    lax.fori_loop(0, tc // CHUNK, chunk_body, 0)


def _scan_call(r, k, v, kk, b, lw, bdm, tri, *, tc, nb):
    bsz, seq, d = r.shape
    assert bsz % nb == 0 and seq % tc == 0 and tc % CHUNK == 0
    tile = pl.BlockSpec((nb, tc, d), lambda bi, i: (bi, i, 0))
    const = lambda bi, i: (0, 0)
    return pl.pallas_call(
        functools.partial(_scan_kernel, tc=tc, nb=nb),
        grid=(bsz // nb, seq // tc),
        in_specs=[tile] * 6 + [pl.BlockSpec(bdm.shape, const), pl.BlockSpec(tri.shape, const)],
        out_specs=tile,
        out_shape=jax.ShapeDtypeStruct((bsz, seq, d), BF16),
        scratch_shapes=[pltpu.VMEM((nb * (d // GW), GW, GW), F32)],
        compiler_params=pltpu.CompilerParams(dimension_semantics=("arbitrary", "arbitrary"),
                                             vmem_limit_bytes=VMEM_LIMIT),
        name="rwkv_scan",
    )(r, k, v, kk, b, lw, bdm, tri)


def _post_kernel(o_ref, bv_ref, sr_ref, sg_ref, mc_ref, x_ref, mod_ref, gg_ref, gb_ref, ones_ref,
                 wr_ref, wo_ref, fg_ref, out_ref, *, last):
    o = o_ref[...].astype(F32)
    inv = 1.0 / HEAD
    mu = _segsum(o, ones_ref[...]) * inv
    cen = o - mu
    var = _segsum(cen * cen, ones_ref[...]) * inv
    on = cen * lax.rsqrt(var + GN_EPS) * gg_ref[...] + gb_ref[...]
    y = ((on + bv_ref[...].astype(F32)) * sr_ref[...].astype(F32)).astype(BF16)
    m = sg_ref[...].astype(F32) * _dot(y, wr_ref[...]) + mc_ref[...].astype(F32)
    out = _dot(m.astype(BF16), wo_ref[...])
    xn = x_ref[...] + mod_ref[2:3, :] * out
    if last:
        ms = jnp.mean(xn * xn, axis=-1, keepdims=True)
        xn = xn * lax.rsqrt(ms + RMS_EPS) * fg_ref[...]
    out_ref[...] = xn


def _post_call(o, bv, sr, sg, mc, x, mod3, gn_g, gn_b, ones_bd, w_r, w_o, final_g, *, tm, last):
    bsz, seq, d = x.shape
    const = lambda b, i: (0, 0)
    row = pl.BlockSpec((1, d), const)
    tile = pl.BlockSpec((None, tm, d), lambda b, i: (b, i, 0))
    return pl.pallas_call(
        functools.partial(_post_kernel, last=last),
        grid=(bsz, seq // tm),
        in_specs=[tile] * 6 + [pl.BlockSpec((None, 3, d), lambda b, i: (b, 0, 0)), row, row,
                               pl.BlockSpec(ones_bd.shape, const),
                               pl.BlockSpec(w_r.shape, const), pl.BlockSpec(w_o.shape, const), row],
        out_specs=tile,
        out_shape=jax.ShapeDtypeStruct((bsz, seq, d), F32),
        compiler_params=pltpu.CompilerParams(dimension_semantics=("arbitrary", "arbitrary"),
                                             vmem_limit_bytes=VMEM_LIMIT),
        name="merge_out",
    )(o, bv, sr, sg, mc, x, mod3, gn_g, gn_b, ones_bd, w_r, w_o, final_g)


def kernel(x, c, ada_w, ada_b, norm_g, w_in, conv_k, conv_b, conv_ln_g, conv_ln_b, w_conv_out,
           rwkv_mu, rwkv_w0, rwkv_w2, rwkv_a0, rwkv_a2, rwkv_k_k, rwkv_k_a, rwkv_r_k,
           rwkv_gn_g, rwkv_gn_b, w_rwkv_out, w_out, final_g):
    depth = ada_w.shape[0]
    bsz, seq, d = x.shape
    lora_w = rwkv_w2.shape[1]
    lora_a = rwkv_a2.shape[1]
    tsw = 3 * d + lora_w + lora_a
    assert rwkv_w2.shape[2] == d and lora_w == lora_a and d % GW == 0 and seq % 256 == 0
    row = lambda t: t.reshape(1, -1).astype(F32)

    idx = jnp.arange(GW) // HEAD
    bdm = (idx[:, None] == idx[None, :]).astype(BF16)
    tri = (jnp.arange(CHUNK)[None, :] <= jnp.arange(CHUNK)[:, None]).astype(BF16)

    for l in range(depth):
        wl = w_in[l]
        wc = jnp.concatenate([wl[:, 0:3 * d], wl[:, 4 * d + tsw:5 * d + tsw]], axis=1).astype(BF16)
        wts = wl[:, 3 * d:3 * d + tsw].astype(BF16)
        wg = jnp.concatenate([wl[:, 3 * d + tsw:4 * d + tsw], wl[:, 5 * d + tsw:6 * d + tsw]], axis=1).astype(BF16)
        zeros = jnp.zeros((lora_w, d), F32)
        w2 = jnp.concatenate([jnp.concatenate([rwkv_w2[l], zeros], axis=1),
                              jnp.concatenate([zeros, rwkv_a2[l]], axis=1)], axis=0).astype(BF16)

        mod3 = _mod_call(c, ada_w[l], ada_b[l]).reshape(bsz, 3, d)
        g = row(norm_g[l])
        m_conv = _conv_call(x, mod3, g, wc, conv_k[l], row(conv_b[l]), row(conv_ln_g[l]), row(conv_ln_b[l]),
                            w_conv_out[l].astype(BF16), tm=512, sub=256)
        r, k2, v, kk, b, lw, bv, sr, sg = _prep_call(
            x, mod3, g, wts, wg, row(rwkv_mu[l]), w2, row(rwkv_w0[l]), row(rwkv_a0[l]),
            row(rwkv_k_k[l]), row(rwkv_k_a[l]), row(rwkv_r_k[l]), bdm, tm=256)
        o = _scan_call(r, k2, v, kk, b, lw, bdm, tri, tc=256, nb=2)
        x = _post_call(o, bv, sr, sg, m_conv, x, mod3, row(rwkv_gn_g[l]), row(rwkv_gn_b[l]), bdm,
                       w_rwkv_out[l].astype(BF16), w_out[l].astype(BF16), row(final_g),
                       tm=256, last=(l == depth - 1))
    return x
```
